```python
import math
import jax, jax.numpy as jnp
from jax import lax
import numpy as np

D_MODEL = 2048
BATCH = 2
SEQ = 4096
DEPTH = 1
DEC_BATCH = 32
DEC_SEQ = 1
PAST_LEN = 8192
PAGE_SIZE = 128

MIX_WIDTH = D_MODEL
ATT_WIDTH = MIX_WIDTH // 2
GM_WIDTH = MIX_WIDTH - ATT_WIDTH
N_HEADS = 8
HEAD_DIM = ATT_WIDTH // (2 * N_HEADS)
VAL_DIM = 2 * HEAD_DIM
QK_COLS = N_HEADS * 2 * HEAD_DIM
V_COLS = N_HEADS * VAL_DIM
GM_GROUPS = 8
GM_GROUP_DIM = GM_WIDTH // GM_GROUPS
CHUNK = 128
IN_COLS = 2 * QK_COLS + V_COLS + 2 * GM_WIDTH
D_FF = -(-8 * D_MODEL // (3 * 256)) * 256
Q_BLOCK = 128
SCALE = HEAD_DIM ** -0.5
ALPHA = (2.0 * DEPTH) ** 0.25
BETA = (8.0 * DEPTH) ** -0.25
LN_EPS = 1e-5

kernel_name = "hymba_diffattn_gmlp_deepnorm_step"


def alibi_slopes():
    return jnp.asarray(np.array([2.0 ** (-8.0 * (h + 1) / N_HEADS) for h in range(N_HEADS)], dtype=np.float32))


def layer_norm(x, g, b):
    xf = x.astype(jnp.float32)
    mu = jnp.mean(xf, axis=-1, keepdims=True)
    var = jnp.mean(jnp.square(xf - mu), axis=-1, keepdims=True)
    y = (xf - mu) * lax.rsqrt(var + LN_EPS) * g.astype(jnp.float32) + b.astype(jnp.float32)
    return y.astype(x.dtype)


def rms_norm(x, g):
    xf = x.astype(jnp.float32)
    y = xf * lax.rsqrt(jnp.mean(jnp.square(xf), axis=-1, keepdims=True) + LN_EPS) * g.astype(jnp.float32)
    return y.astype(x.dtype)


def in_projection(x, w_in):
    b, s = x.shape[:2]
    z = jnp.einsum("bsd,de->bse", x, w_in)
    q = z[..., :QK_COLS].reshape(b, s, N_HEADS, 2, HEAD_DIM)
    k = z[..., QK_COLS:2 * QK_COLS].reshape(b, s, N_HEADS, 2, HEAD_DIM)
    v = z[..., 2 * QK_COLS:2 * QK_COLS + V_COLS].reshape(b, s, N_HEADS, VAL_DIM)
    g = jax.nn.gelu(z[..., 2 * QK_COLS + V_COLS:])
    return q, k, v, g[..., :GM_WIDTH], g[..., GM_WIDTH:]


def alibi_logits(scores, q_pos, k_pos):
    dist = (q_pos[:, None] - k_pos[None, :]).astype(jnp.float32)
    logits = scores.astype(jnp.float32) * SCALE - alibi_slopes()[None, :, None, None, None] * dist
    return jnp.where(dist >= 0, logits, -jnp.inf)


def diff_weights(logits, lam):
    p = jax.nn.softmax(logits, axis=-1)
    return p[:, :, 0] - lam * p[:, :, 1]


def diff_lambda(lq1, lk1, lq2, lk2, lambda_init):
    f32 = jnp.float32
    return (jnp.exp(jnp.sum(lq1.astype(f32) * lk1.astype(f32)))
            - jnp.exp(jnp.sum(lq2.astype(f32) * lk2.astype(f32))) + lambda_init)


def prompt_attention(q, k, v, lam):
    b, s = q.shape[:2]
    nb = s // Q_BLOCK
    qb = q.reshape(b, nb, Q_BLOCK, N_HEADS, 2, HEAD_DIM).transpose(1, 0, 2, 3, 4, 5)
    k_pos = jnp.arange(s)

    def block(args):
        q_blk, start = args
        scores = jnp.einsum("bqhcd,bkhcd->bhcqk", q_blk, k)
        q_pos = start + jnp.arange(Q_BLOCK)
        w = diff_weights(alibi_logits(scores, q_pos, k_pos), lam).astype(v.dtype)
        return jnp.einsum("bhqk,bkhe->bqhe", w, v)

    o = lax.map(block, (qb, jnp.arange(nb) * Q_BLOCK))
    return o.transpose(1, 0, 2, 3, 4).reshape(b, s, N_HEADS, VAL_DIM)


def sample_attention(q, k_new, v_new, k_pool, v_pool, page_table, lam):
    db, l = q.shape[:2]
    past = page_table.shape[1] * PAGE_SIZE
    k_past = k_pool[page_table].reshape(db, past, N_HEADS, 2, HEAD_DIM)
    v_past = v_pool[page_table].reshape(db, past, N_HEADS, VAL_DIM)
    s_past = jnp.einsum("bqhcd,bkhcd->bhcqk", q, k_past)
    s_new = jnp.einsum("bqhcd,bkhcd->bhcqk", q, k_new)
    scores = jnp.concatenate([s_past.astype(jnp.float32), s_new.astype(jnp.float32)], axis=-1)
    q_pos = past + jnp.arange(l)
    k_pos = jnp.arange(past + l)
    w = diff_weights(alibi_logits(scores, q_pos, k_pos), lam).astype(v_new.dtype)
    return (jnp.einsum("bhqk,bkhe->bqhe", w[..., :past], v_past)
            + jnp.einsum("bhqk,bkhe->bqhe", w[..., past:], v_new))


def spatial_gate(vn, w_s, b_s):
    b, l, _ = vn.shape
    c = min(l, CHUNK)
    vr = vn.reshape(b, l // c, c, GM_GROUPS, GM_GROUP_DIM)
    w = jnp.tril(w_s[:, :c, :c])
    gate = jnp.einsum("gts,bnsgd->bntgd", w, vr) + b_s[:, :c].T[None, None, :, :, None]
    return gate.reshape(b, l, GM_WIDTH)


def merge_heads(o_att, u, gate, w_out, subln_g, lambda_init):
    b, l = u.shape[:2]
    att = (rms_norm(o_att, subln_g) * (1.0 - lambda_init)).reshape(b, l, ATT_WIDTH)
    cat = jnp.concatenate([att, u * gate], axis=-1)
    return jnp.einsum("bse,ed->bsd", cat, w_out)


def swiglu(h, w_ffn_in, w_ffn_out):
    gu = jnp.einsum("bsd,df->bsf", h, w_ffn_in)
    return jnp.einsum("bsf,fd->bsd", jax.nn.silu(gu[..., :D_FF]) * gu[..., D_FF:], w_ffn_out)


def post_block(x, mix, ln1_g, ln1_b, w_ffn_in, w_ffn_out, ln2_g, ln2_b):
    h = layer_norm(ALPHA * x + mix, ln1_g, ln1_b)
    return layer_norm(ALPHA * h + swiglu(h, w_ffn_in, w_ffn_out), ln2_g, ln2_b)


def setup_inputs(seed: int = 0) -> dict:
    key = jax.random.key(seed)
    ks = jax.random.split(key, 24)
    n_pages = PAST_LEN // PAGE_SIZE
    n_used = DEC_BATCH * n_pages
    n_phys = n_used + (n_used + 3) // 4
    f32 = jnp.float32
    x_prompt = jax.random.normal(ks[0], (BATCH, SEQ, D_MODEL), f32)
    x_sample = jax.random.normal(ks[1], (DEC_BATCH, DEC_SEQ, D_MODEL), f32)
    cache_k = jax.random.normal(ks[2], (DEPTH, n_phys, PAGE_SIZE, N_HEADS, 2 * HEAD_DIM), f32)
    cache_v = BETA * jax.random.normal(ks[3], (DEPTH, n_phys, PAGE_SIZE, N_HEADS, 2 * HEAD_DIM), f32)
    page_table = jax.random.permutation(ks[4], n_phys)[:n_used].reshape(DEC_BATCH, n_pages).astype(jnp.int32)
    col_scale = jnp.concatenate([jnp.ones((2 * QK_COLS,), f32), jnp.full((V_COLS,), BETA, f32),
                                 jnp.full((GM_WIDTH,), BETA, f32), jnp.ones((GM_WIDTH,), f32)])
    w_in = jax.random.normal(ks[5], (DEPTH, D_MODEL, IN_COLS), f32) * (D_MODEL ** -0.5) * col_scale
    w_out = jax.random.normal(ks[6], (DEPTH, MIX_WIDTH, D_MODEL), f32) * (MIX_WIDTH ** -0.5) * BETA
    lambda_q1 = 0.1 * jax.random.normal(ks[7], (DEPTH, HEAD_DIM), f32)
    lambda_k1 = 0.1 * jax.random.normal(ks[8], (DEPTH, HEAD_DIM), f32)
    lambda_q2 = 0.1 * jax.random.normal(ks[9], (DEPTH, HEAD_DIM), f32)
    lambda_k2 = 0.1 * jax.random.normal(ks[10], (DEPTH, HEAD_DIM), f32)
    subln_g = 1.0 + 0.02 * jax.random.normal(ks[11], (DEPTH, VAL_DIM), f32)
    gm_ln_g = 1.0 + 0.02 * jax.random.normal(ks[12], (DEPTH, GM_WIDTH), f32)
    gm_ln_b = 0.02 * jax.random.normal(ks[13], (DEPTH, GM_WIDTH), f32)
    gm_ws = jax.random.normal(ks[14], (DEPTH, GM_GROUPS, CHUNK, CHUNK), f32) * (CHUNK ** -0.5)
    gm_bs = 1.0 + 0.02 * jax.random.normal(ks[15], (DEPTH, GM_GROUPS, CHUNK), f32)
    ln1_g = 1.0 + 0.02 * jax.random.normal(ks[16], (DEPTH, D_MODEL), f32)
    ln1_b = 0.02 * jax.random.normal(ks[17], (DEPTH, D_MODEL), f32)
    ln2_g = 1.0 + 0.02 * jax.random.normal(ks[18], (DEPTH, D_MODEL), f32)
    ln2_b = 0.02 * jax.random.normal(ks[19], (DEPTH, D_MODEL), f32)
    w_ffn_in = jax.random.normal(ks[20], (DEPTH, D_MODEL, 2 * D_FF), f32) * (D_MODEL ** -0.5) * BETA
    w_ffn_out = jax.random.normal(ks[21], (DEPTH, D_FF, D_MODEL), f32) * (D_FF ** -0.5) * BETA
    return {"x_prompt": x_prompt, "x_sample": x_sample, "cache_k": cache_k, "cache_v": cache_v,
            "page_table": page_table, "w_in": w_in, "w_out": w_out,
            "lambda_q1": lambda_q1, "lambda_k1": lambda_k1, "lambda_q2": lambda_q2, "lambda_k2": lambda_k2,
            "subln_g": subln_g, "gm_ln_g": gm_ln_g, "gm_ln_b": gm_ln_b, "gm_ws": gm_ws, "gm_bs": gm_bs,
            "ln1_g": ln1_g, "ln1_b": ln1_b, "ln2_g": ln2_g, "ln2_b": ln2_b,
            "w_ffn_in": w_ffn_in, "w_ffn_out": w_ffn_out}


def reference(x_prompt, x_sample, cache_k, cache_v, page_table, w_in, w_out,
              lambda_q1, lambda_k1, lambda_q2, lambda_k2, subln_g, gm_ln_g, gm_ln_b,
              gm_ws, gm_bs, ln1_g, ln1_b, ln2_g, ln2_b, w_ffn_in, w_ffn_out):
    xp, xs = x_prompt, x_sample
    kp_rows, vp_rows, ks_rows, vs_rows, gv_rows = [], [], [], [], []
    for l in range(DEPTH):
        lambda_init = 0.8 - 0.6 * math.exp(-0.3 * l)
        lam = diff_lambda(lambda_q1[l], lambda_k1[l], lambda_q2[l], lambda_k2[l], lambda_init)

        q, k, v, u, vg = in_projection(xp, w_in[l])
        vn = layer_norm(vg, gm_ln_g[l], gm_ln_b[l])
        o = prompt_attention(q, k, v, lam)
        mix = merge_heads(o, u, spatial_gate(vn, gm_ws[l], gm_bs[l]), w_out[l], subln_g[l], lambda_init)
        xp = post_block(xp, mix, ln1_g[l], ln1_b[l], w_ffn_in[l], w_ffn_out[l], ln2_g[l], ln2_b[l])
        kp_rows.append(k.reshape(k.shape[0], k.shape[1], N_HEADS, 2 * HEAD_DIM))
        vp_rows.append(v)

        q, k, v, u, vg = in_projection(xs, w_in[l])
        vn = layer_norm(vg, gm_ln_g[l], gm_ln_b[l])
        o = sample_attention(q, k, v, cache_k[l], cache_v[l], page_table, lam)
        mix = merge_heads(o, u, spatial_gate(vn, gm_ws[l], gm_bs[l]), w_out[l], subln_g[l], lambda_init)
        xs = post_block(xs, mix, ln1_g[l], ln1_b[l], w_ffn_in[l], w_ffn_out[l], ln2_g[l], ln2_b[l])
        ks_rows.append(k.reshape(k.shape[0], k.shape[1], N_HEADS, 2 * HEAD_DIM))
        vs_rows.append(v)
        gv_rows.append(vn)

    return (xp, xs, jnp.stack(kp_rows), jnp.stack(vp_rows), jnp.stack(ks_rows), jnp.stack(vs_rows), jnp.stack(gv_rows))
```

```python
import functools
import math

import jax
import jax.numpy as jnp
import numpy as np
from jax import lax
from jax.experimental import pallas as pl
from jax.experimental.pallas import tpu as pltpu

D_MODEL = 2048
DEPTH = 1
PAGE_SIZE = 128
ATT_WIDTH = D_MODEL // 2
GM_WIDTH = D_MODEL - ATT_WIDTH
N_HEADS = 8
HEAD_DIM = ATT_WIDTH // (2 * N_HEADS)
VAL_DIM = 2 * HEAD_DIM
QK_COLS = N_HEADS * 2 * HEAD_DIM
V_COLS = N_HEADS * VAL_DIM
GM_GROUPS = 8
GM_GROUP_DIM = GM_WIDTH // GM_GROUPS
CHUNK = 128
IN_COLS = 2 * QK_COLS + V_COLS + 2 * GM_WIDTH
D_FF = -(-8 * D_MODEL // (3 * 256)) * 256
SCALE = HEAD_DIM ** -0.5
ALPHA = (2.0 * DEPTH) ** 0.25
LN_EPS = 1e-5

V7X_VMEM_BYTES = 64 * 1024 * 1024
VMEM_LIMIT_BYTES = V7X_VMEM_BYTES - 8 * 1024 * 1024
LANES = 128

F32 = jnp.float32
BF16 = jnp.bfloat16
NEG_INF = float("-inf")


def _params(*semantics):
    return pltpu.CompilerParams(dimension_semantics=semantics, vmem_limit_bytes=VMEM_LIMIT_BYTES)


def _layer_norm(x, g, b):
    mu = jnp.mean(x, axis=-1, keepdims=True)
    xc = x - mu
    var = jnp.mean(xc * xc, axis=-1, keepdims=True)
    return xc * lax.rsqrt(var + LN_EPS) * g + b


def _alibi_slope(head_i32):
    return pltpu.bitcast((126 - head_i32) << 23, F32)


def _inproj_kernel(x_ref, w_ref, lng_ref, lnb_ref, ws_ref, bs_ref, *refs, chunked):
    if chunked:
        q_ref, kf_ref, vf_ref, kb_ref, vb_ref, ug_ref, xb_sc, u_sc = refs
    else:
        q_ref, kf_ref, vf_ref, ug_ref, vn_ref, xb_sc, u_sc = refs
    j = pl.program_id(1)

    @pl.when(j == 0)
    def _():
        xb_sc[...] = x_ref[...].astype(BF16)

    z = jnp.dot(xb_sc[...], w_ref[...], preferred_element_type=F32)

    @pl.when(j == 0)
    def _():
        q_ref[...] = (z * SCALE).astype(q_ref.dtype)

    @pl.when(j == 1)
    def _():
        kf_ref[...] = z
        if chunked:
            kb_ref[...] = z.astype(BF16)

    @pl.when(j == 2)
    def _():
        vf_ref[...] = z
        if chunked:
            vb_ref[...] = z.astype(BF16)

    @pl.when(j == 3)
    def _():
        u_sc[...] = jax.nn.gelu(z)

    @pl.when(j == 4)
    def _():
        vn = _layer_norm(jax.nn.gelu(z), lng_ref[...], lnb_ref[...])
        if chunked:
            rows = lax.broadcasted_iota(jnp.int32, (CHUNK, CHUNK), 0)
            cols = lax.broadcasted_iota(jnp.int32, (CHUNK, CHUNK), 1)
            vnb = vn.astype(BF16)
            for g in range(GM_GROUPS):
                w_tril = jnp.where(cols <= rows, ws_ref[g], 0.0).astype(BF16)
                bias = bs_ref[:, g:g + 1]
                lo = g * GM_GROUP_DIM
                for c in range(vn.shape[0] // CHUNK):
                    r0 = c * CHUNK
                    gate = jnp.dot(w_tril, vnb[r0:r0 + CHUNK, lo:lo + GM_GROUP_DIM],
                                   preferred_element_type=F32) + bias
                    ug_ref[r0:r0 + CHUNK, lo:lo + GM_GROUP_DIM] = (
                        u_sc[r0:r0 + CHUNK, lo:lo + GM_GROUP_DIM] * gate).astype(BF16)
        else:
            vn_ref[...] = vn
            ug_ref[...] = (u_sc[...] * (ws_ref[...] * vn + bs_ref[...])).astype(BF16)


def _in_projection(x, w_in_bf, ln_g, ln_b, ws, bs, *, tm, chunked, q_dtype):
    m = x.shape[0]
    tn = QK_COLS
    row_blk = lambda i, j: (i, 0)
    const2 = lambda i, j: (0, 0)
    if chunked:
        ws_spec = pl.BlockSpec((GM_GROUPS, CHUNK, CHUNK), lambda i, j: (0, 0, 0))
        bs_spec = pl.BlockSpec((CHUNK, GM_GROUPS), const2)
    else:
        ws_spec = pl.BlockSpec((1, GM_WIDTH), const2)
        bs_spec = pl.BlockSpec((1, GM_WIDTH), const2)
    out_blk = pl.BlockSpec((tm, tn), row_blk)
    out_dtypes = [q_dtype, F32, F32] + ([BF16, BF16, BF16] if chunked else [BF16, F32])
    return pl.pallas_call(
        functools.partial(_inproj_kernel, chunked=chunked),
        grid=(m // tm, IN_COLS // tn),
        in_specs=[pl.BlockSpec((tm, D_MODEL), row_blk),
                  pl.BlockSpec((D_MODEL, tn), lambda i, j: (0, j)),
                  pl.BlockSpec((1, GM_WIDTH), const2),
                  pl.BlockSpec((1, GM_WIDTH), const2),
                  ws_spec, bs_spec],
        out_specs=[out_blk] * len(out_dtypes),
        out_shape=[jax.ShapeDtypeStruct((m, tn), dt) for dt in out_dtypes],
        scratch_shapes=[pltpu.VMEM((tm, D_MODEL), BF16), pltpu.VMEM((tm, GM_WIDTH), F32)],
        compiler_params=_params("parallel", "arbitrary"),
        name="in_projection_chunked" if chunked else "in_projection_step",
    )(x, w_in_bf, ln_g, ln_b, ws, bs)


def _diff_lambda(lq1_ref, lk1_ref, lq2_ref, lk2_ref, lambda_init):
    a = jnp.sum(lq1_ref[...] * lk1_ref[...], axis=-1, keepdims=True)
    b = jnp.sum(lq2_ref[...] * lk2_ref[...], axis=-1, keepdims=True)
    return jnp.exp(a) - jnp.exp(b) + lambda_init


def _diff_combine_norm(acc, l, lam, g, n, lambda_init):
    o = acc[:n] / l[:n] - lam * (acc[n:] / l[n:])
    ms = jnp.mean(o * o, axis=-1, keepdims=True)
    return o * lax.rsqrt(ms + LN_EPS) * g * (1.0 - lambda_init)


def _prompt_attn_kernel(q_ref, k_ref, v_ref, lq1_ref, lk1_ref, lq2_ref, lk2_ref, g_ref,
                        o_ref, qbd_sc, m_sc, l_sc, acc_sc, *, tq, lambda_init):
    head = pl.program_id(1)
    qi = pl.program_id(2)

    q = q_ref[...]
    lane = lax.broadcasted_iota(jnp.int32, q.shape, 1)
    qbd_sc[0:tq, :] = jnp.where(lane < HEAD_DIM, q, jnp.zeros_like(q))
    qbd_sc[tq:2 * tq, :] = jnp.where(lane >= HEAD_DIM, q, jnp.zeros_like(q))

    col = lax.broadcasted_iota(jnp.int32, (1, tq), 1)
    slope = _alibi_slope(jnp.zeros((1, tq), jnp.int32) + head)

    def scores(kv_start, rel_start):
        kt = k_ref[pl.ds(kv_start, tq), :]
        s = lax.dot_general(qbd_sc[...], kt, (((1,), (1,)), ((), ())),
                            preferred_element_type=F32)
        return s + (col + rel_start).astype(F32) * slope

    d0 = pl.multiple_of(qi * tq, tq)
    s = scores(d0, 0)
    r2 = lax.broadcasted_iota(jnp.int32, (2 * tq, tq), 0)
    c2 = lax.broadcasted_iota(jnp.int32, (2 * tq, tq), 1)
    row_in_tile = jnp.where(r2 >= tq, r2 - tq, r2)
    s = jnp.where(c2 > row_in_tile, NEG_INF, s)
    m0 = jnp.max(s, axis=-1, keepdims=True)
    p = jnp.exp(s - m0)
    m_sc[...] = m0
    l_sc[...] = jnp.sum(p, axis=-1, keepdims=True)
    acc_sc[...] = jnp.dot(p.astype(BF16), v_ref[pl.ds(d0, tq), :], preferred_element_type=F32)

    def body(jt, carry):
        start = pl.multiple_of(jt * tq, tq)
        s = scores(start, (jt - qi) * tq)
        m_prev = m_sc[...]
        m_new = jnp.maximum(m_prev, jnp.max(s, axis=-1, keepdims=True))
        alpha = jnp.exp(m_prev - m_new)
        p = jnp.exp(s - m_new)
        l_sc[...] = alpha * l_sc[...] + jnp.sum(p, axis=-1, keepdims=True)
        acc_sc[...] = alpha * acc_sc[...] + jnp.dot(
            p.astype(BF16), v_ref[pl.ds(start, tq), :], preferred_element_type=F32)
        m_sc[...] = m_new
        return carry

    lax.fori_loop(0, qi, body, 0)

    lam = _diff_lambda(lq1_ref, lk1_ref, lq2_ref, lk2_ref, lambda_init)
    o_ref[...] = _diff_combine_norm(acc_sc[...], l_sc[...], lam, g_ref[...], tq,
                                    lambda_init).astype(o_ref.dtype)


def _prompt_attention(q, k, v, lq1, lk1, lq2, lk2, subln_g, *, batch, seq, tq, lambda_init):
    vec = pl.BlockSpec((1, HEAD_DIM), lambda b, h, i: (0, 0))
    kv_spec = pl.BlockSpec((None, seq, VAL_DIM), lambda b, h, i: (b, 0, h))
    q_spec = pl.BlockSpec((None, tq, VAL_DIM), lambda b, h, i: (b, i, h))
    return pl.pallas_call(
        functools.partial(_prompt_attn_kernel, tq=tq, lambda_init=lambda_init),
        grid=(batch, N_HEADS, seq // tq),
        in_specs=[q_spec, kv_spec, kv_spec, vec, vec, vec, vec,
                  pl.BlockSpec((1, VAL_DIM), lambda b, h, i: (0, 0))],
        out_specs=q_spec,
        out_shape=jax.ShapeDtypeStruct((batch, seq, ATT_WIDTH), BF16),
        scratch_shapes=[pltpu.VMEM((2 * tq, VAL_DIM), BF16),
                        pltpu.VMEM((2 * tq, 1), F32),
                        pltpu.VMEM((2 * tq, 1), F32),
                        pltpu.VMEM((2 * tq, VAL_DIM), F32)],
        compiler_params=_params("parallel", "parallel", "arbitrary"),
        name="prompt_attention",
    )(q, k, v, lq1, lk1, lq2, lk2, subln_g)


def _paged_attn_kernel(pt_ref, *refs, pages_per_step, lambda_init, past_len):
    g_pages = pages_per_step
    q_ref = refs[0]
    k_refs = refs[1:1 + g_pages]
    v_refs = refs[1 + g_pages:1 + 2 * g_pages]
    (knew_ref, vnew_ref, lq1_ref, lk1_ref, lq2_ref, lk2_ref, g_ref,
     o_ref, s_sc, m_sc, l_sc, acc_sc) = refs[1 + 2 * g_pages:]
    del pt_ref
    j = pl.program_id(1)
    n_rows = 2 * N_HEADS
    page_cols = PAGE_SIZE * N_HEADS

    q = q_ref[...]
    lane = lax.broadcasted_iota(jnp.int32, q.shape, 1)
    q_stack = jnp.concatenate([jnp.where(lane < HEAD_DIM, q, 0.0),
                               jnp.where(lane >= HEAD_DIM, q, 0.0)], axis=0)
    q_bf = q_stack.astype(BF16)

    row = lax.broadcasted_iota(jnp.int32, (n_rows, page_cols), 0)
    colp = lax.broadcasted_iota(jnp.int32, (n_rows, page_cols), 1)
    own_head = (colp & (N_HEADS - 1)) == (row & (N_HEADS - 1))
    slope = _alibi_slope(row & (N_HEADS - 1))
    tok = colp >> 3

    @pl.when(j == 0)
    def _():
        m_sc[...] = jnp.full(m_sc.shape, NEG_INF, F32)
        l_sc[...] = jnp.zeros(l_sc.shape, F32)
        acc_sc[...] = jnp.zeros(acc_sc.shape, F32)

    for g in range(g_pages):
        k2d = k_refs[g][...].reshape(page_cols, VAL_DIM).astype(BF16)
        s = lax.dot_general(q_bf, k2d, (((1,), (1,)), ((), ())), preferred_element_type=F32)
        k_pos = tok + (j * g_pages + g) * PAGE_SIZE
        s = s - slope * (past_len - k_pos).astype(F32)
        s_sc[:, g * page_cols:(g + 1) * page_cols] = jnp.where(own_head, s, NEG_INF)

    s_all = s_sc[...]
    m_prev = m_sc[...]
    m_new = jnp.maximum(m_prev, jnp.max(s_all, axis=-1, keepdims=True))
    alpha = jnp.exp(m_prev - m_new)
    p = jnp.exp(s_all - m_new)
    l_sc[...] = alpha * l_sc[...] + jnp.sum(p, axis=-1, keepdims=True)
    pb = p.astype(BF16)
    acc = alpha * acc_sc[...]
    for g in range(g_pages):
        v2d = v_refs[g][...].reshape(page_cols, VAL_DIM).astype(BF16)
        acc = acc + jnp.dot(pb[:, g * page_cols:(g + 1) * page_cols], v2d,
                            preferred_element_type=F32)
    acc_sc[...] = acc
    m_sc[...] = m_new

    @pl.when(j == pl.num_programs(1) - 1)
    def _():
        k_new = jnp.concatenate([knew_ref[...], knew_ref[...]], axis=0)
        v_new = jnp.concatenate([vnew_ref[...], vnew_ref[...]], axis=0)
        s_new = jnp.sum(q_stack * k_new, axis=-1, keepdims=True)
        m_fin = jnp.maximum(m_sc[...], s_new)
        a_fin = jnp.exp(m_sc[...] - m_fin)
        p_new = jnp.exp(s_new - m_fin)
        l_fin = a_fin * l_sc[...] + p_new
        acc_fin = a_fin * acc_sc[...] + p_new * v_new
        lam = _diff_lambda(lq1_ref, lk1_ref, lq2_ref, lk2_ref, lambda_init)
        o_ref[...] = _diff_combine_norm(acc_fin, l_fin, lam, g_ref[...], N_HEADS, lambda_init)


def _paged_attention(page_table, q, cache_k, cache_v, k_new, v_new, lq1, lk1, lq2, lk2,
                     subln_g, *, pages_per_step, lambda_init):
    batch, n_pages = page_table.shape
    g_pages = pages_per_step
    past_len = n_pages * PAGE_SIZE
    tok_blk = pl.BlockSpec((None, N_HEADS, VAL_DIM), lambda b, j, pt: (b, 0, 0))
    vec = pl.BlockSpec((1, HEAD_DIM), lambda b, j, pt: (0, 0))

    def page_spec(g):
        return pl.BlockSpec((None, PAGE_SIZE, N_HEADS, VAL_DIM),
                            lambda b, j, pt, g=g: (pt[b, j * g_pages + g], 0, 0, 0))

    page_specs = [page_spec(g) for g in range(g_pages)]
    grid_spec = pltpu.PrefetchScalarGridSpec(
        num_scalar_prefetch=1,
        grid=(batch, n_pages // g_pages),
        in_specs=[tok_blk] + page_specs + page_specs + [tok_blk, tok_blk, vec, vec, vec, vec,
                  pl.BlockSpec((1, VAL_DIM), lambda b, j, pt: (0, 0))],
        out_specs=tok_blk,
        scratch_shapes=[pltpu.VMEM((2 * N_HEADS, g_pages * PAGE_SIZE * N_HEADS), F32),
                        pltpu.VMEM((2 * N_HEADS, 1), F32),
                        pltpu.VMEM((2 * N_HEADS, 1), F32),
                        pltpu.VMEM((2 * N_HEADS, VAL_DIM), F32)],
    )
    return pl.pallas_call(
        functools.partial(_paged_attn_kernel, pages_per_step=g_pages, lambda_init=lambda_init,
                          past_len=past_len),
        grid_spec=grid_spec,
        out_shape=jax.ShapeDtypeStruct((batch, N_HEADS, VAL_DIM), F32),
        compiler_params=_params("parallel", "arbitrary"),
        name="paged_attention",
    )(page_table, q, *([cache_k] * g_pages), *([cache_v] * g_pages), k_new, v_new,
      lq1, lk1, lq2, lk2, subln_g)


def _outproj_kernel(att_ref, ug_ref, x_ref, wa_ref, wg_ref, g_ref, b_ref, h_ref, hb_ref):
    mix = jnp.dot(att_ref[...].astype(BF16), wa_ref[...], preferred_element_type=F32)
    mix = mix + jnp.dot(ug_ref[...], wg_ref[...], preferred_element_type=F32)
    h = _layer_norm(ALPHA * x_ref[...] + mix, g_ref[...], b_ref[...])
    h_ref[...] = h
    hb_ref[...] = h.astype(BF16)


def _out_projection(att, ug, x, w_out_bf, ln_g, ln_b, *, tm):
    m = x.shape[0]
    row_blk = lambda i: (i, 0)
    const = lambda i: (0, 0)
    return pl.pallas_call(
        _outproj_kernel,
        grid=(m // tm,),
        in_specs=[pl.BlockSpec((tm, ATT_WIDTH), row_blk),
                  pl.BlockSpec((tm, GM_WIDTH), row_blk),
                  pl.BlockSpec((tm, D_MODEL), row_blk),
                  pl.BlockSpec((ATT_WIDTH, D_MODEL), const),
                  pl.BlockSpec((GM_WIDTH, D_MODEL), lambda i: (1, 0)),
                  pl.BlockSpec((1, D_MODEL), const),
                  pl.BlockSpec((1, D_MODEL), const)],
        out_specs=[pl.BlockSpec((tm, D_MODEL), row_blk)] * 2,
        out_shape=[jax.ShapeDtypeStruct((m, D_MODEL), F32),
                   jax.ShapeDtypeStruct((m, D_MODEL), BF16)],
        compiler_params=_params("parallel"),
        name="out_projection",
    )(att, ug, x, w_out_bf, w_out_bf, ln_g, ln_b)


def _ffn_in_kernel(h_ref, wa_ref, wb_ref, o_ref):
    h = h_ref[...]
    a = jnp.dot(h, wa_ref[...], preferred_element_type=F32)
    b = jnp.dot(h, wb_ref[...], preferred_element_type=F32)
    o_ref[...] = (jax.nn.silu(a) * b).astype(BF16)


def _ffn_in(h_bf, w1_bf, *, tm, tf):
    m = h_bf.shape[0]
    nf = D_FF // tf
    return pl.pallas_call(
        _ffn_in_kernel,
        grid=(m // tm, nf),
        in_specs=[pl.BlockSpec((tm, D_MODEL), lambda i, j: (i, 0)),
                  pl.BlockSpec((D_MODEL, tf), lambda i, j: (0, j)),
                  pl.BlockSpec((D_MODEL, tf), lambda i, j: (0, j + nf))],
        out_specs=pl.BlockSpec((tm, tf), lambda i, j: (i, j)),
        out_shape=jax.ShapeDtypeStruct((m, D_FF), BF16),
        compiler_params=_params("parallel", "arbitrary"),
        name="ffn_in",
    )(h_bf, w1_bf, w1_bf)


def _ffn_out_kernel(g_ref, w_ref, h_ref, lg_ref, lb_ref, y_ref, acc_sc):
    kk = pl.program_id(1)
    part = jnp.dot(g_ref[...], w_ref[...], preferred_element_type=F32)

    @pl.when(kk == 0)
    def _():
        acc_sc[...] = part

    @pl.when(kk > 0)
    def _():
        acc_sc[...] += part

    @pl.when(kk == pl.num_programs(1) - 1)
    def _():
        y_ref[...] = _layer_norm(ALPHA * h_ref[...] + acc_sc[...], lg_ref[...], lb_ref[...])


def _ffn_out(g_bf, w2_bf, h, ln_g, ln_b, *, tm, tk):
    m = h.shape[0]
    return pl.pallas_call(
        _ffn_out_kernel,
        grid=(m // tm, D_FF // tk),
        in_specs=[pl.BlockSpec((tm, tk), lambda i, k: (i, k)),
                  pl.BlockSpec((tk, D_MODEL), lambda i, k: (k, 0)),
                  pl.BlockSpec((tm, D_MODEL), lambda i, k: (i, 0)),
                  pl.BlockSpec((1, D_MODEL), lambda i, k: (0, 0)),
                  pl.BlockSpec((1, D_MODEL), lambda i, k: (0, 0))],
        out_specs=pl.BlockSpec((tm, D_MODEL), lambda i, k: (i, 0)),
        out_shape=jax.ShapeDtypeStruct((m, D_MODEL), F32),
        scratch_shapes=[pltpu.VMEM((tm, D_MODEL), F32)],
        compiler_params=_params("parallel", "arbitrary"),
        name="ffn_out",
    )(g_bf, w2_bf, h, ln_g, ln_b)


def _post_block(att, ug, x, w_out_bf, ln1_g, ln1_b, w1_bf, w2_bf, ln2_g, ln2_b,
                *, tm_out, tm_in, tf, tm_down, tk_down):
    h, h_bf = _out_projection(att, ug, x, w_out_bf, ln1_g, ln1_b, tm=tm_out)
    g_bf = _ffn_in(h_bf, w1_bf, tm=tm_in, tf=tf)
    return _ffn_out(g_bf, w2_bf, h, ln2_g, ln2_b, tm=tm_down, tk=tk_down)


def kernel(x_prompt, x_sample, cache_k, cache_v, page_table, w_in, w_out, lambda_q1, lambda_k1,
           lambda_q2, lambda_k2, subln_g, gm_ln_g, gm_ln_b, gm_ws, gm_bs, ln1_g, ln1_b, ln2_g,
           ln2_b, w_ffn_in, w_ffn_out):
    batch, seq, _ = x_prompt.shape
    dec_batch, dec_seq, _ = x_sample.shape
    assert dec_seq == 1 and DEPTH == 1
    lyr = 0
    lambda_init = 0.8 - 0.6 * math.exp(-0.3 * lyr)

    w_in_bf = w_in[lyr].astype(BF16)
    w_out_bf = w_out[lyr].astype(BF16)
    w1_bf = w_ffn_in[lyr].astype(BF16)
    w2_bf = w_ffn_out[lyr].astype(BF16)
    row = lambda a: a[lyr].reshape(1, -1)
    lq1, lk1, lq2, lk2 = row(lambda_q1), row(lambda_k1), row(lambda_q2), row(lambda_k2)
    sub_g = row(subln_g)
    lng, lnb = row(gm_ln_g), row(gm_ln_b)
    l1g, l1b, l2g, l2b = row(ln1_g), row(ln1_b), row(ln2_g), row(ln2_b)

    xp = x_prompt.reshape(batch * seq, D_MODEL)
    q_p, k_p, v_p, kb_p, vb_p, ug_p = _in_projection(
        xp, w_in_bf, lng, lnb, gm_ws[lyr], gm_bs[lyr].T, tm=512, chunked=True, q_dtype=BF16)
    att_p = _prompt_attention(
        q_p.reshape(batch, seq, QK_COLS), kb_p.reshape(batch, seq, QK_COLS),
        vb_p.reshape(batch, seq, V_COLS), lq1, lk1, lq2, lk2, sub_g,
        batch=batch, seq=seq, tq=512, lambda_init=lambda_init)
    y_p = _post_block(att_p.reshape(batch * seq, ATT_WIDTH), ug_p, xp, w_out_bf, l1g, l1b,
                      w1_bf, w2_bf, l2g, l2b,
                      tm_out=512, tm_in=1024, tf=512, tm_down=512, tk_down=1408)

    xs = x_sample.reshape(dec_batch, D_MODEL)
    ws0 = jnp.repeat(gm_ws[lyr][:, 0, 0], GM_GROUP_DIM).reshape(1, GM_WIDTH)
    bs0 = jnp.repeat(gm_bs[lyr][:, 0], GM_GROUP_DIM).reshape(1, GM_WIDTH)
    q_s, k_s, v_s, ug_s, vn_s = _in_projection(
        xs, w_in_bf, lng, lnb, ws0, bs0, tm=dec_batch, chunked=False, q_dtype=F32)
    heads = lambda a: a.reshape(dec_batch, N_HEADS, VAL_DIM)
    att_s = _paged_attention(page_table, heads(q_s), cache_k[lyr], cache_v[lyr], heads(k_s),
                             heads(v_s), lq1, lk1, lq2, lk2, sub_g,
                             pages_per_step=8, lambda_init=lambda_init)
    y_s = _post_block(att_s.reshape(dec_batch, ATT_WIDTH), ug_s, xs, w_out_bf, l1g, l1b,
                      w1_bf, w2_bf, l2g, l2b,
                      tm_out=dec_batch, tm_in=dec_batch, tf=512, tm_down=dec_batch,
                      tk_down=1408)

    kv5 = lambda a, b_, s_: a.reshape(1, b_, s_, N_HEADS, VAL_DIM)
    return (y_p.reshape(batch, seq, D_MODEL),
            y_s.reshape(dec_batch, dec_seq, D_MODEL),
            kv5(k_p, batch, seq), kv5(v_p, batch, seq),
            kv5(k_s, dec_batch, dec_seq), kv5(v_s, dec_batch, dec_seq),
            vn_s.reshape(1, dec_batch, dec_seq, GM_WIDTH))
```

```python
import functools
import math

import jax
import jax.numpy as jnp
from jax import lax
from jax.experimental import pallas as pl
from jax.experimental.pallas import tpu as pltpu

D_MODEL = 2048
DEPTH = 1
PAGE_SIZE = 128
ATT_WIDTH = D_MODEL // 2
GM_WIDTH = D_MODEL - ATT_WIDTH
N_HEADS = 8
HEAD_DIM = ATT_WIDTH // (2 * N_HEADS)
VAL_DIM = 2 * HEAD_DIM
QK_COLS = N_HEADS * 2 * HEAD_DIM
V_COLS = N_HEADS * VAL_DIM
GM_GROUPS = 8
GM_GROUP_DIM = GM_WIDTH // GM_GROUPS
CHUNK = 128
IN_COLS = 2 * QK_COLS + V_COLS + 2 * GM_WIDTH
D_FF = -(-8 * D_MODEL // (3 * 256)) * 256
SCALE = HEAD_DIM ** -0.5
ALPHA = (2.0 * DEPTH) ** 0.25
LN_EPS = 1e-5
LOG2E = math.log2(math.e)

V7X_VMEM_BYTES = 64 * 1024 * 1024
VMEM_LIMIT_BYTES = V7X_VMEM_BYTES - 8 * 1024 * 1024
LANES = 128
SOFTMAX_COLS = 2 * LANES

F32 = jnp.float32
BF16 = jnp.bfloat16
NEG_INF = float("-inf")


def _params(*semantics):
    return pltpu.CompilerParams(dimension_semantics=semantics, vmem_limit_bytes=VMEM_LIMIT_BYTES)


def _layer_norm(x, g, b):
    mu = jnp.mean(x, axis=-1, keepdims=True)
    xc = x - mu
    var = jnp.mean(xc * xc, axis=-1, keepdims=True)
    return xc * lax.rsqrt(var + LN_EPS) * g + b


def _alibi_slope(head_i32):
    return pltpu.bitcast((126 - head_i32) << 23, F32)


def _row_halves(tm):
    if tm % 64 == 0:
        return [(0, tm // 2), (tm // 2, tm // 2)]
    return [(0, tm)]


def _inproj_kernel(x_ref, w_ref, lng_ref, lnb_ref, ws_ref, bs_ref, *refs, chunked, q_scale):
    if chunked:
        q_ref, kf_ref, vf_ref, kb_ref, vb_ref, ug_ref, xb_sc, u_sc = refs
    else:
        q_ref, kf_ref, vf_ref, ug_ref, vn_ref, xb_sc, u_sc = refs
    j = pl.program_id(1)

    @pl.when(j == 0)
    def _():
        xb_sc[...] = x_ref[...].astype(BF16)

    def proj():
        return jnp.dot(xb_sc[...], w_ref[...], preferred_element_type=F32)

    @pl.when(j == 0)
    def _():
        q_ref[...] = (proj() * q_scale).astype(q_ref.dtype)

    @pl.when(j == 1)
    def _():
        z = proj()
        kf_ref[...] = z
        if chunked:
            kb_ref[...] = z.astype(BF16)

    @pl.when(j == 2)
    def _():
        z = proj()
        vf_ref[...] = z
        if chunked:
            vb_ref[...] = z.astype(BF16)

    @pl.when(j == 3)
    def _():
        u_sc[...] = jax.nn.gelu(proj())

    @pl.when(j == 4)
    def _():
        vn = _layer_norm(jax.nn.gelu(proj()), lng_ref[...], lnb_ref[...])
        if chunked:
            rows = lax.broadcasted_iota(jnp.int32, (CHUNK, CHUNK), 0)
            cols = lax.broadcasted_iota(jnp.int32, (CHUNK, CHUNK), 1)
            vnb = vn.astype(BF16)
            for g in range(GM_GROUPS):
                w_tril = jnp.where(cols <= rows, ws_ref[g], 0.0).astype(BF16)
                bias = bs_ref[:, g:g + 1]
                lo = g * GM_GROUP_DIM
                for c in range(vn.shape[0] // CHUNK):
                    r0 = c * CHUNK
                    gate = jnp.dot(w_tril, vnb[r0:r0 + CHUNK, lo:lo + GM_GROUP_DIM],
                                   preferred_element_type=F32) + bias
                    ug_ref[r0:r0 + CHUNK, lo:lo + GM_GROUP_DIM] = (
                        u_sc[r0:r0 + CHUNK, lo:lo + GM_GROUP_DIM] * gate).astype(BF16)
        else:
            vn_ref[...] = vn
            ug_ref[...] = (u_sc[...] * (ws_ref[...] * vn + bs_ref[...])).astype(BF16)


def _in_projection(x, w_in_bf, ln_g, ln_b, ws, bs, *, tm, chunked, q_dtype, q_scale):
    m = x.shape[0]
    tn = QK_COLS
    row_blk = lambda i, j: (i, 0)
    const2 = lambda i, j: (0, 0)
    if chunked:
        ws_spec = pl.BlockSpec((GM_GROUPS, CHUNK, CHUNK), lambda i, j: (0, 0, 0))
        bs_spec = pl.BlockSpec((CHUNK, GM_GROUPS), const2)
    else:
        ws_spec = pl.BlockSpec((1, GM_WIDTH), const2)
        bs_spec = pl.BlockSpec((1, GM_WIDTH), const2)
    out_blk = pl.BlockSpec((tm, tn), row_blk)
    out_dtypes = [q_dtype, F32, F32] + ([BF16, BF16, BF16] if chunked else [BF16, F32])
    return pl.pallas_call(
        functools.partial(_inproj_kernel, chunked=chunked, q_scale=q_scale),
        grid=(m // tm, IN_COLS // tn),
        in_specs=[pl.BlockSpec((tm, D_MODEL), row_blk),
                  pl.BlockSpec((D_MODEL, tn), lambda i, j: (0, j)),
                  pl.BlockSpec((1, GM_WIDTH), const2),
                  pl.BlockSpec((1, GM_WIDTH), const2),
                  ws_spec, bs_spec],
        out_specs=[out_blk] * len(out_dtypes),
        out_shape=[jax.ShapeDtypeStruct((m, tn), dt) for dt in out_dtypes],
        scratch_shapes=[pltpu.VMEM((tm, D_MODEL), BF16), pltpu.VMEM((tm, GM_WIDTH), F32)],
        compiler_params=_params("parallel", "arbitrary"),
        name="in_projection_chunked" if chunked else "in_projection_step",
    )(x, w_in_bf, ln_g, ln_b, ws, bs)


def _diff_lambda(lq1_ref, lk1_ref, lq2_ref, lk2_ref, lambda_init):
    a = jnp.sum(lq1_ref[...] * lk1_ref[...], axis=-1, keepdims=True)
    b = jnp.sum(lq2_ref[...] * lk2_ref[...], axis=-1, keepdims=True)
    return jnp.exp(a) - jnp.exp(b) + lambda_init


def _prompt_attn_kernel(q_ref, k_ref, v_ref, lq1_ref, lk1_ref, lq2_ref, lk2_ref, g_ref,
                        o_ref, qbd_sc, vt_sc, p_sc, a_sc, m_sc, l_sc, acc_sc,
                        *, tq, lambda_init):
    head = pl.program_id(1)
    qi = pl.program_id(2)
    n_kv = vt_sc.shape[0]
    cols = SOFTMAX_COLS

    @pl.when(qi == 0)
    def _():
        for t in range(n_kv):
            vt_sc[t] = v_ref[t * tq:(t + 1) * tq, :].astype(F32).T.astype(BF16)

    q = q_ref[...]
    lane = lax.broadcasted_iota(jnp.int32, q.shape, 1)
    qbd_sc[0:tq, :] = jnp.where(lane < HEAD_DIM, q, jnp.zeros_like(q))
    qbd_sc[tq:2 * tq, :] = jnp.where(lane >= HEAD_DIM, q, jnp.zeros_like(q))

    key_row = lax.broadcasted_iota(jnp.int32, (tq, LANES), 0)
    slope2 = _alibi_slope(jnp.zeros((tq, LANES), jnp.int32) + head) * LOG2E
    mask_row = lax.broadcasted_iota(jnp.int32, (tq, cols), 0)
    mask_col = lax.broadcasted_iota(jnp.int32, (tq, cols), 1)

    def kv_tile(jt, rel_start, first):
        start = pl.multiple_of(jt * tq, tq)
        s = lax.dot_general(k_ref[pl.ds(start, tq), :], qbd_sc[...], (((1,), (1,)), ((), ())),
                            preferred_element_type=F32)
        bias = (key_row + rel_start).astype(F32) * slope2
        bias = jnp.concatenate([bias] * (cols // LANES), axis=1)
        for cb in range(2 * tq // cols):
            lo = cb * cols
            sc = s[:, lo:lo + cols] + bias
            if first:
                q_in_tile = mask_col + (lo % tq)
                sc = jnp.where(mask_row > q_in_tile, NEG_INF, sc)
            m_cur = jnp.max(sc, axis=0, keepdims=True)
            if first:
                m_new = m_cur
            else:
                m_prev = m_sc[:, lo:lo + cols]
                m_new = jnp.maximum(m_prev, m_cur)
            p = jnp.exp2(sc - m_new)
            p_sum = jnp.sum(p, axis=0, keepdims=True)
            if first:
                l_sc[:, lo:lo + cols] = p_sum
            else:
                alpha = jnp.exp2(m_prev - m_new)
                a_sc[:, lo:lo + cols] = alpha
                l_sc[:, lo:lo + cols] = alpha * l_sc[:, lo:lo + cols] + p_sum
            m_sc[:, lo:lo + cols] = m_new
            p_sc[:, lo:lo + cols] = p.astype(BF16)
        pv = jnp.dot(vt_sc[jt], p_sc[...], preferred_element_type=F32)
        if first:
            acc_sc[...] = pv
        else:
            acc_sc[...] = acc_sc[...] * a_sc[...] + pv

    kv_tile(qi, 0, True)

    def body(jt, carry):
        kv_tile(jt, (jt - qi) * tq, False)
        return carry

    lax.fori_loop(0, qi, body, 0)

    lam = _diff_lambda(lq1_ref, lk1_ref, lq2_ref, lk2_ref, lambda_init)
    inv_l = 1.0 / l_sc[...]
    acc = acc_sc[...]
    o = acc[:, :tq] * inv_l[:, :tq] - lam * (acc[:, tq:] * inv_l[:, tq:])
    ms = jnp.mean(o * o, axis=0, keepdims=True)
    y = (o * lax.rsqrt(ms + LN_EPS)).T * g_ref[...] * (1.0 - lambda_init)
    o_ref[...] = y.astype(o_ref.dtype)


def _prompt_attention(q, k, v, lq1, lk1, lq2, lk2, subln_g, *, batch, seq, tq, lambda_init):
    vec = pl.BlockSpec((1, HEAD_DIM), lambda b, h, i: (0, 0))
    kv_spec = pl.BlockSpec((None, seq, VAL_DIM), lambda b, h, i: (b, 0, h))
    q_spec = pl.BlockSpec((None, tq, VAL_DIM), lambda b, h, i: (b, i, h))
    return pl.pallas_call(
        functools.partial(_prompt_attn_kernel, tq=tq, lambda_init=lambda_init),
        grid=(batch, N_HEADS, seq // tq),
        in_specs=[q_spec, kv_spec, kv_spec, vec, vec, vec, vec,
                  pl.BlockSpec((1, VAL_DIM), lambda b, h, i: (0, 0))],
        out_specs=q_spec,
        out_shape=jax.ShapeDtypeStruct((batch, seq, ATT_WIDTH), BF16),
        scratch_shapes=[pltpu.VMEM((2 * tq, VAL_DIM), BF16),
                        pltpu.VMEM((seq // tq, VAL_DIM, tq), BF16),
                        pltpu.VMEM((tq, 2 * tq), BF16),
                        pltpu.VMEM((1, 2 * tq), F32),
                        pltpu.VMEM((1, 2 * tq), F32),
                        pltpu.VMEM((1, 2 * tq), F32),
                        pltpu.VMEM((VAL_DIM, 2 * tq), F32)],
        compiler_params=_params("parallel", "parallel", "arbitrary"),
        name="prompt_attention",
    )(q, k, v, lq1, lk1, lq2, lk2, subln_g)


def _paged_attn_kernel(pt_ref, *refs, pages_per_step, lambda_init, past_len):
    g_pages = pages_per_step
    q_ref = refs[0]
    k_refs = refs[1:1 + g_pages]
    v_refs = refs[1 + g_pages:1 + 2 * g_pages]
    (knew_ref, vnew_ref, lq1_ref, lk1_ref, lq2_ref, lk2_ref, g_ref,
     o_ref, s_sc, m_sc, l_sc, acc_sc) = refs[1 + 2 * g_pages:]
    del pt_ref
    j = pl.program_id(1)
    n_rows = 2 * N_HEADS
    page_cols = PAGE_SIZE * N_HEADS

    q = q_ref[...]
    lane = lax.broadcasted_iota(jnp.int32, q.shape, 1)
    q_stack = jnp.concatenate([jnp.where(lane < HEAD_DIM, q, 0.0),
                               jnp.where(lane >= HEAD_DIM, q, 0.0)], axis=0)
    q_bf = q_stack.astype(BF16)

    row = lax.broadcasted_iota(jnp.int32, (n_rows, page_cols), 0)
    colp = lax.broadcasted_iota(jnp.int32, (n_rows, page_cols), 1)
    own_head = (colp & (N_HEADS - 1)) == (row & (N_HEADS - 1))
    slope = _alibi_slope(row & (N_HEADS - 1))
    tok = colp >> 3

    @pl.when(j == 0)
    def _():
        m_sc[...] = jnp.full(m_sc.shape, NEG_INF, F32)
        l_sc[...] = jnp.zeros(l_sc.shape, F32)
        acc_sc[...] = jnp.zeros(acc_sc.shape, F32)

    for g in range(g_pages):
        k2d = k_refs[g][...].reshape(page_cols, VAL_DIM).astype(BF16)
        s = lax.dot_general(q_bf, k2d, (((1,), (1,)), ((), ())), preferred_element_type=F32)
        k_pos = tok + (j * g_pages + g) * PAGE_SIZE
        s = s - slope * (past_len - k_pos).astype(F32)
        s_sc[:, g * page_cols:(g + 1) * page_cols] = jnp.where(own_head, s, NEG_INF)

    s_all = s_sc[...]
    m_prev = m_sc[...]
    m_new = jnp.maximum(m_prev, jnp.max(s_all, axis=-1, keepdims=True))
    alpha = jnp.exp(m_prev - m_new)
    p = jnp.exp(s_all - m_new)
    l_sc[...] = alpha * l_sc[...] + jnp.sum(p, axis=-1, keepdims=True)
    pb = p.astype(BF16)
    acc = alpha * acc_sc[...]
    for g in range(g_pages):
        v2d = v_refs[g][...].reshape(page_cols, VAL_DIM).astype(BF16)
        acc = acc + jnp.dot(pb[:, g * page_cols:(g + 1) * page_cols], v2d,
                            preferred_element_type=F32)
    acc_sc[...] = acc
    m_sc[...] = m_new

    @pl.when(j == pl.num_programs(1) - 1)
    def _():
        k_new = jnp.concatenate([knew_ref[...], knew_ref[...]], axis=0)
        v_new = jnp.concatenate([vnew_ref[...], vnew_ref[...]], axis=0)
        s_new = jnp.sum(q_stack * k_new, axis=-1, keepdims=True)
        m_fin = jnp.maximum(m_sc[...], s_new)
        a_fin = jnp.exp(m_sc[...] - m_fin)
        p_new = jnp.exp(s_new - m_fin)
        l_fin = a_fin * l_sc[...] + p_new
        acc_fin = a_fin * acc_sc[...] + p_new * v_new
        lam = _diff_lambda(lq1_ref, lk1_ref, lq2_ref, lk2_ref, lambda_init)
        n = N_HEADS
        o = acc_fin[:n] / l_fin[:n] - lam * (acc_fin[n:] / l_fin[n:])
        ms = jnp.mean(o * o, axis=-1, keepdims=True)
        o_ref[...] = o * lax.rsqrt(ms + LN_EPS) * g_ref[...] * (1.0 - lambda_init)


def _paged_attention(page_table, q, cache_k, cache_v, k_new, v_new, lq1, lk1, lq2, lk2,
                     subln_g, *, pages_per_step, lambda_init):
    batch, n_pages = page_table.shape
    g_pages = pages_per_step
    past_len = n_pages * PAGE_SIZE
    tok_blk = pl.BlockSpec((None, N_HEADS, VAL_DIM), lambda b, j, pt: (b, 0, 0))
    vec = pl.BlockSpec((1, HEAD_DIM), lambda b, j, pt: (0, 0))

    def page_spec(g):
        return pl.BlockSpec((None, PAGE_SIZE, N_HEADS, VAL_DIM),
                            lambda b, j, pt, g=g: (pt[b, j * g_pages + g], 0, 0, 0))

    page_specs = [page_spec(g) for g in range(g_pages)]
    grid_spec = pltpu.PrefetchScalarGridSpec(
        num_scalar_prefetch=1,
        grid=(batch, n_pages // g_pages),
        in_specs=[tok_blk] + page_specs + page_specs + [tok_blk, tok_blk, vec, vec, vec, vec,
                  pl.BlockSpec((1, VAL_DIM), lambda b, j, pt: (0, 0))],
        out_specs=tok_blk,
        scratch_shapes=[pltpu.VMEM((2 * N_HEADS, g_pages * PAGE_SIZE * N_HEADS), F32),
                        pltpu.VMEM((2 * N_HEADS, 1), F32),
                        pltpu.VMEM((2 * N_HEADS, 1), F32),
                        pltpu.VMEM((2 * N_HEADS, VAL_DIM), F32)],
    )
    return pl.pallas_call(
        functools.partial(_paged_attn_kernel, pages_per_step=g_pages, lambda_init=lambda_init,
                          past_len=past_len),
        grid_spec=grid_spec,
        out_shape=jax.ShapeDtypeStruct((batch, N_HEADS, VAL_DIM), F32),
        compiler_params=_params("parallel", "arbitrary"),
        name="paged_attention",
    )(page_table, q, *([cache_k] * g_pages), *([cache_v] * g_pages), k_new, v_new,
      lq1, lk1, lq2, lk2, subln_g)


def _outproj_kernel(att_ref, ug_ref, x_ref, wa_ref, wg_ref, g_ref, b_ref, h_ref, hb_ref):
    for r0, nr in _row_halves(att_ref.shape[0]):
        rows = slice(r0, r0 + nr)
        mix = jnp.dot(att_ref[rows, :].astype(BF16), wa_ref[...], preferred_element_type=F32)
        mix = mix + jnp.dot(ug_ref[rows, :], wg_ref[...], preferred_element_type=F32)
        h = _layer_norm(ALPHA * x_ref[rows, :] + mix, g_ref[...], b_ref[...])
        h_ref[rows, :] = h
        hb_ref[rows, :] = h.astype(BF16)


def _out_projection(att, ug, x, w_out_bf, ln_g, ln_b, *, tm):
    m = x.shape[0]
    row_blk = lambda i: (i, 0)
    const = lambda i: (0, 0)
    return pl.pallas_call(
        _outproj_kernel,
        grid=(m // tm,),
        in_specs=[pl.BlockSpec((tm, ATT_WIDTH), row_blk),
                  pl.BlockSpec((tm, GM_WIDTH), row_blk),
                  pl.BlockSpec((tm, D_MODEL), row_blk),
                  pl.BlockSpec((ATT_WIDTH, D_MODEL), const),
                  pl.BlockSpec((GM_WIDTH, D_MODEL), lambda i: (1, 0)),
                  pl.BlockSpec((1, D_MODEL), const),
                  pl.BlockSpec((1, D_MODEL), const)],
        out_specs=[pl.BlockSpec((tm, D_MODEL), row_blk)] * 2,
        out_shape=[jax.ShapeDtypeStruct((m, D_MODEL), F32),
                   jax.ShapeDtypeStruct((m, D_MODEL), BF16)],
        compiler_params=_params("parallel"),
        name="out_projection",
    )(att, ug, x, w_out_bf, w_out_bf, ln_g, ln_b)


def _ffn_in_kernel(h_ref, wa_ref, wb_ref, o_ref, wa_sc, wb_sc):
    @pl.when(pl.program_id(1) == 0)
    def _():
        wa_sc[...] = wa_ref[...].astype(BF16)
        wb_sc[...] = wb_ref[...].astype(BF16)

    h = h_ref[...]
    a = jnp.dot(h, wa_sc[...], preferred_element_type=F32)
    b = jnp.dot(h, wb_sc[...], preferred_element_type=F32)
    o_ref[...] = (jax.nn.silu(a) * b).astype(BF16)


def _ffn_in(h_bf, w1, *, tm, tf):
    m = h_bf.shape[0]
    nf = D_FF // tf
    return pl.pallas_call(
        _ffn_in_kernel,
        grid=(nf, m // tm),
        in_specs=[pl.BlockSpec((tm, D_MODEL), lambda j, i: (i, 0)),
                  pl.BlockSpec((D_MODEL, tf), lambda j, i: (0, j)),
                  pl.BlockSpec((D_MODEL, tf), lambda j, i: (0, j + nf))],
        out_specs=pl.BlockSpec((tm, tf), lambda j, i: (i, j)),
        out_shape=jax.ShapeDtypeStruct((m, D_FF), BF16),
        scratch_shapes=[pltpu.VMEM((D_MODEL, tf), BF16), pltpu.VMEM((D_MODEL, tf), BF16)],
        compiler_params=_params("parallel", "arbitrary"),
        name="ffn_in",
    )(h_bf, w1, w1)


def _ffn_out_kernel(g_ref, w_ref, h_ref, lg_ref, lb_ref, y_ref, acc_sc):
    kk = pl.program_id(1)
    last = pl.num_programs(1) - 1

    @pl.when(kk == 0)
    def _():
        acc_sc[...] = jnp.dot(g_ref[...], w_ref[...], preferred_element_type=F32)

    @pl.when(jnp.logical_and(kk > 0, kk < last))
    def _():
        acc_sc[...] += jnp.dot(g_ref[...], w_ref[...], preferred_element_type=F32)

    @pl.when(kk == last)
    def _():
        for r0, nr in _row_halves(g_ref.shape[0]):
            rows = slice(r0, r0 + nr)
            tot = acc_sc[rows, :] + jnp.dot(g_ref[rows, :], w_ref[...],
                                            preferred_element_type=F32)
            y_ref[rows, :] = _layer_norm(ALPHA * h_ref[rows, :] + tot, lg_ref[...], lb_ref[...])


def _ffn_out(g_bf, w2_bf, h, ln_g, ln_b, *, tm, tk):
    m = h.shape[0]
    assert D_FF // tk >= 2
    return pl.pallas_call(
        _ffn_out_kernel,
        grid=(m // tm, D_FF // tk),
        in_specs=[pl.BlockSpec((tm, tk), lambda i, k: (i, k)),
                  pl.BlockSpec((tk, D_MODEL), lambda i, k: (k, 0)),
                  pl.BlockSpec((tm, D_MODEL), lambda i, k: (i, 0)),
                  pl.BlockSpec((1, D_MODEL), lambda i, k: (0, 0)),
                  pl.BlockSpec((1, D_MODEL), lambda i, k: (0, 0))],
        out_specs=pl.BlockSpec((tm, D_MODEL), lambda i, k: (i, 0)),
        out_shape=jax.ShapeDtypeStruct((m, D_MODEL), F32),
        scratch_shapes=[pltpu.VMEM((tm, D_MODEL), F32)],
        compiler_params=_params("parallel", "arbitrary"),
        name="ffn_out",
    )(g_bf, w2_bf, h, ln_g, ln_b)


def _post_block(att, ug, x, w_out_bf, ln1_g, ln1_b, w1, w2_bf, ln2_g, ln2_b,
                *, tm_out, tm_in, tf, tm_down, tk_down):
    h, h_bf = _out_projection(att, ug, x, w_out_bf, ln1_g, ln1_b, tm=tm_out)
    g_bf = _ffn_in(h_bf, w1, tm=tm_in, tf=tf)
    return _ffn_out(g_bf, w2_bf, h, ln2_g, ln2_b, tm=tm_down, tk=tk_down)


def kernel(x_prompt, x_sample, cache_k, cache_v, page_table, w_in, w_out, lambda_q1, lambda_k1,
           lambda_q2, lambda_k2, subln_g, gm_ln_g, gm_ln_b, gm_ws, gm_bs, ln1_g, ln1_b, ln2_g,
           ln2_b, w_ffn_in, w_ffn_out):
    batch, seq, _ = x_prompt.shape
    dec_batch, dec_seq, _ = x_sample.shape
    assert dec_seq == 1 and DEPTH == 1
    lyr = 0
    lambda_init = 0.8 - 0.6 * math.exp(-0.3 * lyr)

    w_in_bf = w_in[lyr].astype(BF16)
    w_out_bf = w_out[lyr].astype(BF16)
    w1 = w_ffn_in[lyr]
    w2_bf = w_ffn_out[lyr].astype(BF16)
    row = lambda a: a[lyr].reshape(1, -1)
    lq1, lk1, lq2, lk2 = row(lambda_q1), row(lambda_k1), row(lambda_q2), row(lambda_k2)
    sub_g = row(subln_g)
    lng, lnb = row(gm_ln_g), row(gm_ln_b)
    l1g, l1b, l2g, l2b = row(ln1_g), row(ln1_b), row(ln2_g), row(ln2_b)

    xp = x_prompt.reshape(batch * seq, D_MODEL)
    q_p, k_p, v_p, kb_p, vb_p, ug_p = _in_projection(
        xp, w_in_bf, lng, lnb, gm_ws[lyr], gm_bs[lyr].T, tm=512, chunked=True, q_dtype=BF16,
        q_scale=SCALE * LOG2E)
    att_p = _prompt_attention(
        q_p.reshape(batch, seq, QK_COLS), kb_p.reshape(batch, seq, QK_COLS),
        vb_p.reshape(batch, seq, V_COLS), lq1, lk1, lq2, lk2, sub_g,
        batch=batch, seq=seq, tq=512, lambda_init=lambda_init)
    y_p = _post_block(att_p.reshape(batch * seq, ATT_WIDTH), ug_p, xp, w_out_bf, l1g, l1b,
                      w1, w2_bf, l2g, l2b,
                      tm_out=512, tm_in=1024, tf=512, tm_down=512, tk_down=1408)

    xs = x_sample.reshape(dec_batch, D_MODEL)
    ws0 = jnp.repeat(gm_ws[lyr][:, 0, 0], GM_GROUP_DIM).reshape(1, GM_WIDTH)
    bs0 = jnp.repeat(gm_bs[lyr][:, 0], GM_GROUP_DIM).reshape(1, GM_WIDTH)
    q_s, k_s, v_s, ug_s, vn_s = _in_projection(
        xs, w_in_bf, lng, lnb, ws0, bs0, tm=dec_batch, chunked=False, q_dtype=F32,
        q_scale=SCALE)
    heads = lambda a: a.reshape(dec_batch, N_HEADS, VAL_DIM)
    att_s = _paged_attention(page_table, heads(q_s), cache_k[lyr], cache_v[lyr], heads(k_s),
                             heads(v_s), lq1, lk1, lq2, lk2, sub_g,
                             pages_per_step=8, lambda_init=lambda_init)
    y_s = _post_block(att_s.reshape(dec_batch, ATT_WIDTH), ug_s, xs, w_out_bf, l1g, l1b,
                      w1, w2_bf, l2g, l2b,
                      tm_out=dec_batch, tm_in=dec_batch, tf=512, tm_down=dec_batch,
                      tk_down=1408)

    kv5 = lambda a, b_, s_: a.reshape(1, b_, s_, N_HEADS, VAL_DIM)
    return (y_p.reshape(batch, seq, D_MODEL),
            y_s.reshape(dec_batch, dec_seq, D_MODEL),
            kv5(k_p, batch, seq), kv5(v_p, batch, seq),
            kv5(k_s, dec_batch, dec_seq), kv5(v_s, dec_batch, dec_seq),
            vn_s.reshape(1, dec_batch, dec_seq, GM_WIDTH))
```
